```python
import math
import jax, jax.numpy as jnp
from jax import lax
import numpy as np

D_MODEL = 1024
BATCH = 8
SEQ = 2048
DEPTH = 2

N_EVEN = (DEPTH + 1) // 2
N_ODD = DEPTH // 2

D_FF = 2816

MIX_WIDTH = D_MODEL
ATTN_WIDTH = MIX_WIDTH // 2
DIFF_HEADS = 4
DIFF_HEAD_DIM = ATTN_WIDTH // (2 * DIFF_HEADS)
DIFF_V_DIM = 2 * DIFF_HEAD_DIM
POOL_WIDTH = MIX_WIDTH - ATTN_WIDTH
POOL_GROUPS = 4
POOL_GROUP_DIM = POOL_WIDTH // POOL_GROUPS
POOL_WINDOWS = (2, 4, 8, 16)
EVEN_IN = 3 * ATTN_WIDTH + POOL_WIDTH

CONV_WIDTH = 3
CONV_DIM = D_MODEL

Q_BLOCK = 128
EPS = 1e-6

kernel_name = "hybrid_diffattn_pool_shortconv_macaron"


def rms_norm(x, g):
    xf = x.astype(jnp.float32)
    y = xf * lax.rsqrt(jnp.mean(xf * xf, axis=-1, keepdims=True) + EPS)
    return (y * g.astype(jnp.float32)).astype(x.dtype)


def swiglu(h, w_gate, w_up, w_down):
    return (jax.nn.silu(h @ w_gate) * (h @ w_up)) @ w_down


def diff_attention(q, k, v, lam):
    S = q.shape[1]
    scale = DIFF_HEAD_DIM ** -0.5
    outs = []
    for start in range(0, S, Q_BLOCK):
        end = min(start + Q_BLOCK, S)
        qb = q[:, start:end]
        kb = k[:, :end]
        vb = v[:, :end]
        s = jnp.einsum('bqhcd,bkhcd->bhcqk', qb, kb).astype(jnp.float32) * scale
        qpos = start + jnp.arange(end - start)
        kpos = jnp.arange(end)
        mask = qpos[:, None] >= kpos[None, :]
        s = jnp.where(mask, s, -jnp.inf)
        p = jax.nn.softmax(s, axis=-1)
        w = p[:, :, 0] - lam * p[:, :, 1]
        outs.append(jnp.einsum('bhqk,bkhe->bqhe', w.astype(vb.dtype), vb))
    return jnp.concatenate(outs, axis=1)


def multiscale_pool(u, w_pool, pool_scale):
    B, S, G, C = u.shape
    uf = u.astype(jnp.float32)
    cs = jnp.cumsum(uf, axis=1)
    cs0 = jnp.concatenate([jnp.zeros((B, 1, G, C), jnp.float32), cs], axis=1)
    win = jnp.array(POOL_WINDOWS, dtype=jnp.int32)
    t = jnp.arange(S, dtype=jnp.int32)
    lo = jnp.maximum(t[:, None] + 1 - win[None, :], 0)
    count = jnp.minimum(t[:, None] + 1, win[None, :]).astype(jnp.float32)
    lo_vals = cs0[:, lo, jnp.arange(G)[None, :]]
    mean = (cs0[:, 1:] - lo_vals) / count[None, :, :, None]
    pooled = (mean - uf).astype(u.dtype)
    y = jnp.einsum('bsgc,gcd->bsgd', pooled, w_pool)
    return y * pool_scale.reshape(G, C)


def even_mixer(h, w_in, w_out, q_gain, k_gain, subln_gain, lq1, lk1, lq2, lk2,
               w_pool, pool_scale, lam_init):
    B, S, _ = h.shape
    proj = h @ w_in
    q = proj[..., :ATTN_WIDTH].reshape(B, S, DIFF_HEADS, 2, DIFF_HEAD_DIM)
    k = proj[..., ATTN_WIDTH:2 * ATTN_WIDTH].reshape(B, S, DIFF_HEADS, 2, DIFF_HEAD_DIM)
    v = proj[..., 2 * ATTN_WIDTH:3 * ATTN_WIDTH].reshape(B, S, DIFF_HEADS, DIFF_V_DIM)
    u = proj[..., 3 * ATTN_WIDTH:].reshape(B, S, POOL_GROUPS, POOL_GROUP_DIM)
    q = rms_norm(q, q_gain)
    k = rms_norm(k, k_gain)
    lam = (jnp.exp(jnp.sum(lq1.astype(jnp.float32) * lk1.astype(jnp.float32)))
           - jnp.exp(jnp.sum(lq2.astype(jnp.float32) * lk2.astype(jnp.float32)))
           + lam_init)
    o = diff_attention(q, k, v, lam)
    o = rms_norm(o, subln_gain) * (1.0 - lam_init)
    p = multiscale_pool(u, w_pool, pool_scale)
    y = jnp.concatenate([o.reshape(B, S, ATTN_WIDTH), p.reshape(B, S, POOL_WIDTH)], axis=-1)
    return y @ w_out


def odd_mixer(h, w_in, conv_w, w_out):
    bch = h @ w_in
    b_gate = bch[..., :CONV_DIM]
    c_gate = bch[..., CONV_DIM:2 * CONV_DIM]
    hh = bch[..., 2 * CONV_DIM:]
    z = c_gate * hh
    kern = conv_w.reshape(CONV_WIDTH, 1, CONV_DIM).astype(z.dtype)
    z = lax.conv_general_dilated(z, kern, window_strides=(1,),
                                 padding=[(CONV_WIDTH - 1, 0)],
                                 dimension_numbers=('NWC', 'WIO', 'NWC'),
                                 feature_group_count=CONV_DIM)
    return (b_gate * z) @ w_out


def setup_inputs(seed: int = 0) -> dict:
    key = jax.random.key(seed)
    ks = iter(jax.random.split(key, 32))
    f32 = jnp.float32

    def nrm(shape, fan_in):
        return jax.random.normal(next(ks), shape, f32) * (fan_in ** -0.5)

    def gain(shape):
        return jnp.ones(shape, f32) + 0.05 * jax.random.normal(next(ks), shape, f32)

    return {
        "x": jax.random.normal(next(ks), (BATCH, SEQ, D_MODEL), f32),
        "ffn1_norm": gain((DEPTH, D_MODEL)),
        "ffn1_w_gate": nrm((DEPTH, D_MODEL, D_FF), D_MODEL),
        "ffn1_w_up": nrm((DEPTH, D_MODEL, D_FF), D_MODEL),
        "ffn1_w_down": nrm((DEPTH, D_FF, D_MODEL), D_FF),
        "ffn2_norm": gain((DEPTH, D_MODEL)),
        "ffn2_w_gate": nrm((DEPTH, D_MODEL, D_FF), D_MODEL),
        "ffn2_w_up": nrm((DEPTH, D_MODEL, D_FF), D_MODEL),
        "ffn2_w_down": nrm((DEPTH, D_FF, D_MODEL), D_FF),
        "mix_norm": gain((DEPTH, D_MODEL)),
        "even_w_in": nrm((N_EVEN, D_MODEL, EVEN_IN), D_MODEL),
        "even_w_out": nrm((N_EVEN, MIX_WIDTH, D_MODEL), MIX_WIDTH),
        "q_gain": gain((N_EVEN, DIFF_HEAD_DIM)),
        "k_gain": gain((N_EVEN, DIFF_HEAD_DIM)),
        "subln_gain": gain((N_EVEN, DIFF_V_DIM)),
        "lambda_q1": 0.1 * jax.random.normal(next(ks), (N_EVEN, DIFF_HEAD_DIM), f32),
        "lambda_k1": 0.1 * jax.random.normal(next(ks), (N_EVEN, DIFF_HEAD_DIM), f32),
        "lambda_q2": 0.1 * jax.random.normal(next(ks), (N_EVEN, DIFF_HEAD_DIM), f32),
        "lambda_k2": 0.1 * jax.random.normal(next(ks), (N_EVEN, DIFF_HEAD_DIM), f32),
        "w_pool": nrm((N_EVEN, POOL_GROUPS, POOL_GROUP_DIM, POOL_GROUP_DIM), POOL_GROUP_DIM),
        "pool_scale": gain((N_EVEN, POOL_WIDTH)),
        "odd_w_in": nrm((N_ODD, D_MODEL, 3 * CONV_DIM), D_MODEL),
        "conv_w": nrm((N_ODD, CONV_WIDTH, CONV_DIM), CONV_WIDTH),
        "odd_w_out": nrm((N_ODD, CONV_DIM, D_MODEL), CONV_DIM),
    }


def reference(x, ffn1_norm, ffn1_w_gate, ffn1_w_up, ffn1_w_down,
              ffn2_norm, ffn2_w_gate, ffn2_w_up, ffn2_w_down,
              mix_norm, even_w_in, even_w_out, q_gain, k_gain, subln_gain,
              lambda_q1, lambda_k1, lambda_q2, lambda_k2, w_pool, pool_scale,
              odd_w_in, conv_w, odd_w_out):
    for l in range(DEPTH):
        x = x + 0.5 * swiglu(rms_norm(x, ffn1_norm[l]), ffn1_w_gate[l], ffn1_w_up[l], ffn1_w_down[l])
        h = rms_norm(x, mix_norm[l])
        i = l // 2
        if l % 2 == 0:
            lam_init = 0.8 - 0.6 * math.exp(-0.3 * l)
            x = x + even_mixer(h, even_w_in[i], even_w_out[i], q_gain[i], k_gain[i],
                               subln_gain[i], lambda_q1[i], lambda_k1[i],
                               lambda_q2[i], lambda_k2[i], w_pool[i], pool_scale[i],
                               lam_init)
        else:
            x = x + odd_mixer(h, odd_w_in[i], conv_w[i], odd_w_out[i])
        x = x + 0.5 * swiglu(rms_norm(x, ffn2_norm[l]), ffn2_w_gate[l], ffn2_w_up[l], ffn2_w_down[l])
    return x
```

```python
import functools
import math

import jax
import jax.numpy as jnp
from jax import lax
from jax.experimental import pallas as pl
from jax.experimental.pallas import tpu as pltpu

D_MODEL = 1024
D_FF = 2816
DEPTH = 2
ATTN_WIDTH = 512
DIFF_HEADS = 4
DIFF_HEAD_DIM = 64
DIFF_V_DIM = 2 * DIFF_HEAD_DIM
POOL_WIDTH = 512
POOL_GROUPS = 4
POOL_GROUP_DIM = POOL_WIDTH // POOL_GROUPS
POOL_WINDOWS = (2, 4, 8, 16)
EVEN_IN = 3 * ATTN_WIDTH + POOL_WIDTH
CONV_WIDTH = 3
CONV_DIM = D_MODEL
EPS = 1e-6

LANES = 128
SUBLANES = 8
MIB = 1024 * 1024

TOKEN_TILE = 512
FF_CHUNK = 256
Q_TILE = 256
POOL_HALO = 16
CONV_HALO = SUBLANES

F32 = jnp.float32
BF16 = jnp.bfloat16


def _vmem_limit(*nbytes):
    total = sum(nbytes)
    return int(min(60 * MIB, (total * 5 // 4 // MIB + 4) * MIB))


def _resident(shape):
    zeros = (0,) * len(shape)
    return pl.BlockSpec(shape, lambda *_: zeros, pipeline_mode=pl.Buffered(1))


def _rms_scale(x):
    return lax.rsqrt(jnp.mean(x * x, axis=-1, keepdims=True) + EPS)


def _dot(a, b):
    return jnp.dot(a, b, preferred_element_type=F32)


def _ffn_kernel(x_ref, g_ref, wg_ref, wu_ref, wd_ref, o_ref):
    x = x_ref[...]
    xn = (x * _rms_scale(x) * g_ref[...]).astype(BF16)
    acc = jnp.zeros(x.shape, F32)
    for c in range(D_FF // FF_CHUNK):
        sl = slice(c * FF_CHUNK, (c + 1) * FF_CHUNK)
        hg = _dot(xn, wg_ref[:, sl])
        hu = _dot(xn, wu_ref[:, sl])
        a = (hg * jax.nn.sigmoid(hg) * hu).astype(BF16)
        acc = acc + _dot(a, wd_ref[sl, :])
    o_ref[...] = x + 0.5 * acc


def _ffn(x, gain, w_gate, w_up, w_down):
    t, d = x.shape
    tm = TOKEN_TILE
    row = pl.BlockSpec((tm, d), lambda i: (i, 0))
    return pl.pallas_call(
        _ffn_kernel,
        grid=(t // tm,),
        in_specs=[row, _resident((1, d)), _resident((d, D_FF)), _resident((d, D_FF)),
                  _resident((D_FF, d))],
        out_specs=row,
        out_shape=jax.ShapeDtypeStruct((t, d), F32),
        compiler_params=pltpu.CompilerParams(
            dimension_semantics=("parallel",),
            vmem_limit_bytes=_vmem_limit(3 * d * D_FF * 2, 4 * tm * d * 4,
                                         tm * d * 4, 3 * tm * FF_CHUNK * 4)),
        name="ffn",
    )(x, gain, w_gate, w_up, w_down)


def _even_in_kernel(x_ref, g_ref, w_ref, qkg_ref, wp_ref, ps_ref, qkv_ref, p_ref,
                    ubuf, *, tiles_per_seq):
    tm = x_ref.shape[0]
    x = x_ref[...]
    h = (x * _rms_scale(x) * g_ref[...]).astype(BF16)
    proj = _dot(h, w_ref[...])

    lo = lax.broadcasted_iota(jnp.int32, (1, LANES), 1) < DIFF_HEAD_DIM
    for j in range(2 * ATTN_WIDTH // LANES):
        sl = slice(j * LANES, (j + 1) * LANES)
        blk = proj[:, sl]
        sq = blk * blk
        ss_lo = jnp.sum(jnp.where(lo, sq, 0.0), axis=-1, keepdims=True)
        ss_hi = jnp.sum(jnp.where(lo, 0.0, sq), axis=-1, keepdims=True)
        ms = jnp.where(lo, ss_lo, ss_hi) * (1.0 / DIFF_HEAD_DIM)
        qkv_ref[:, sl] = (blk * lax.rsqrt(ms + EPS) * qkg_ref[:, sl]).astype(BF16)
    v_sl = slice(2 * ATTN_WIDTH, 3 * ATTN_WIDTH)
    qkv_ref[:, v_sl] = proj[:, v_sl].astype(BF16)

    seq_tile = pl.program_id(0) % tiles_per_seq

    @pl.when(seq_tile == 0)
    def _():
        ubuf[0:POOL_HALO, :] = jnp.zeros((POOL_HALO, POOL_WIDTH), F32)

    ubuf[POOL_HALO:POOL_HALO + tm, :] = proj[:, 3 * ATTN_WIDTH:]
    pos = seq_tile * tm + lax.broadcasted_iota(jnp.int32, (tm, 1), 0)
    for g, win in enumerate(POOL_WINDOWS):
        sl = slice(g * POOL_GROUP_DIM, (g + 1) * POOL_GROUP_DIM)
        u = ubuf[POOL_HALO:POOL_HALO + tm, sl]
        acc = u
        for j in range(1, win):
            acc = acc + ubuf[POOL_HALO - j:POOL_HALO - j + tm, sl]
        count = jnp.minimum(pos + 1, win).astype(F32)
        pooled = (acc / count - u).astype(BF16)
        y = _dot(pooled, wp_ref[g])
        p_ref[:, sl] = (y * ps_ref[:, sl]).astype(BF16)
    ubuf[0:POOL_HALO, :] = ubuf[tm:tm + POOL_HALO, :]


def _even_in(x, gain, w_in, qk_gain, w_pool, pool_scale, seq_len):
    t, d = x.shape
    tm = TOKEN_TILE
    row = lambda w: pl.BlockSpec((tm, w), lambda i: (i, 0))
    return pl.pallas_call(
        functools.partial(_even_in_kernel, tiles_per_seq=seq_len // tm),
        grid=(t // tm,),
        in_specs=[row(d), _resident((1, d)), _resident((d, EVEN_IN)),
                  _resident((1, 2 * ATTN_WIDTH)),
                  _resident((POOL_GROUPS, POOL_GROUP_DIM, POOL_GROUP_DIM)),
                  _resident((1, POOL_WIDTH))],
        out_specs=[row(3 * ATTN_WIDTH), row(POOL_WIDTH)],
        out_shape=[jax.ShapeDtypeStruct((t, 3 * ATTN_WIDTH), BF16),
                   jax.ShapeDtypeStruct((t, POOL_WIDTH), BF16)],
        scratch_shapes=[pltpu.VMEM((POOL_HALO + tm, POOL_WIDTH), F32)],
        compiler_params=pltpu.CompilerParams(
            dimension_semantics=("arbitrary",),
            vmem_limit_bytes=_vmem_limit(d * EVEN_IN * 2, 2 * tm * d * 4,
                                         2 * tm * EVEN_IN * 2, 2 * tm * EVEN_IN * 4)),
        name="even_in",
    )(x, gain, w_in, qk_gain, w_pool, pool_scale)


def _attn_kernel(lq1_ref, lk1_ref, lq2_ref, lk2_ref, sg_ref, q_ref, k_ref, v_ref, o_ref,
                 *, lam_init):
    seq = q_ref.shape[0]
    lam = (jnp.exp(jnp.sum(lq1_ref[...] * lk1_ref[...], axis=-1, keepdims=True))
           - jnp.exp(jnp.sum(lq2_ref[...] * lk2_ref[...], axis=-1, keepdims=True))
           + lam_init)
    lo = lax.broadcasted_iota(jnp.int32, (1, LANES), 1) < DIFF_HEAD_DIM
    out_gain = sg_ref[...] * (1.0 - lam_init)
    contract_last = (((1,), (1,)), ((), ()))

    def softmax(s, mask):
        s = jnp.where(mask, s, -jnp.inf)
        e = jnp.exp(s - jnp.max(s, axis=-1, keepdims=True))
        return e * (1.0 / jnp.sum(e, axis=-1, keepdims=True))

    for i in range(seq // Q_TILE):
        kv = (i + 1) * Q_TILE
        q = q_ref[i * Q_TILE:(i + 1) * Q_TILE, :]
        q1 = jnp.where(lo, q, jnp.zeros_like(q))
        q2 = jnp.where(lo, jnp.zeros_like(q), q)
        k = k_ref[0:kv, :]
        s1 = lax.dot_general(q1, k, contract_last, preferred_element_type=F32)
        s2 = lax.dot_general(q2, k, contract_last, preferred_element_type=F32)
        rows = i * Q_TILE + lax.broadcasted_iota(jnp.int32, (Q_TILE, 1), 0)
        cols = lax.broadcasted_iota(jnp.int32, (1, kv), 1)
        mask = rows >= cols
        w = (softmax(s1, mask) - lam * softmax(s2, mask)).astype(BF16)
        o = _dot(w, v_ref[0:kv, :])
        o_ref[i * Q_TILE:(i + 1) * Q_TILE, :] = (o * _rms_scale(o) * out_gain).astype(BF16)


def _attention(qkv, lq1, lk1, lq2, lk2, subln_gain, lam_init):
    b, s, _ = qkv.shape
    head = lambda off: pl.BlockSpec((None, s, LANES), lambda bi, hi: (bi, 0, off + hi))
    vec = lambda n: pl.BlockSpec((1, n), lambda bi, hi: (0, 0))
    return pl.pallas_call(
        functools.partial(_attn_kernel, lam_init=lam_init),
        grid=(b, DIFF_HEADS),
        in_specs=[vec(DIFF_HEAD_DIM)] * 4 + [vec(DIFF_V_DIM), head(0), head(DIFF_HEADS),
                                             head(2 * DIFF_HEADS)],
        out_specs=pl.BlockSpec((None, s, LANES), lambda bi, hi: (bi, 0, hi)),
        out_shape=jax.ShapeDtypeStruct((b, s, ATTN_WIDTH), BF16),
        compiler_params=pltpu.CompilerParams(
            dimension_semantics=("parallel", "parallel"),
            vmem_limit_bytes=_vmem_limit(8 * s * LANES * 2, 6 * Q_TILE * s * 4)),
        name="diff_attn",
    )(lq1, lk1, lq2, lk2, subln_gain, qkv, qkv, qkv)


def _even_out_kernel(x_ref, o_ref, p_ref, w_ref, y_ref):
    y = _dot(o_ref[...], w_ref[0:ATTN_WIDTH, :]) + _dot(p_ref[...], w_ref[ATTN_WIDTH:, :])
    y_ref[...] = x_ref[...] + y


def _even_out(x, o, p, w_out):
    t, d = x.shape
    tm = TOKEN_TILE
    row = lambda w: pl.BlockSpec((tm, w), lambda i: (i, 0))
    return pl.pallas_call(
        _even_out_kernel,
        grid=(t // tm,),
        in_specs=[row(d), row(ATTN_WIDTH), row(POOL_WIDTH), _resident((d, d))],
        out_specs=row(d),
        out_shape=jax.ShapeDtypeStruct((t, d), F32),
        compiler_params=pltpu.CompilerParams(
            dimension_semantics=("parallel",),
            vmem_limit_bytes=_vmem_limit(d * d * 2, 6 * tm * d * 4)),
        name="even_out",
    )(x, o, p, w_out)


def _odd_kernel(x_ref, g_ref, wi_ref, cw_ref, wo_ref, y_ref, zbuf, *, tiles_per_seq):
    tm = x_ref.shape[0]
    x = x_ref[...]
    h = (x * _rms_scale(x) * g_ref[...]).astype(BF16)
    bch = _dot(h, wi_ref[...])
    b_gate = bch[:, 0:CONV_DIM]
    z = bch[:, CONV_DIM:2 * CONV_DIM] * bch[:, 2 * CONV_DIM:]

    @pl.when(pl.program_id(0) % tiles_per_seq == 0)
    def _():
        zbuf[0:CONV_HALO, :] = jnp.zeros((CONV_HALO, CONV_DIM), F32)

    zbuf[CONV_HALO:CONV_HALO + tm, :] = z
    conv = cw_ref[CONV_WIDTH - 1:CONV_WIDTH, :] * z
    for j in range(1, CONV_WIDTH):
        tap = cw_ref[CONV_WIDTH - 1 - j:CONV_WIDTH - j, :]
        conv = conv + tap * zbuf[CONV_HALO - j:CONV_HALO - j + tm, :]
    zbuf[0:CONV_HALO, :] = zbuf[tm:tm + CONV_HALO, :]
    y_ref[...] = x + _dot((b_gate * conv).astype(BF16), wo_ref[...])


def _odd_mixer(x, gain, w_in, conv_w, w_out, seq_len):
    t, d = x.shape
    tm = TOKEN_TILE
    row = pl.BlockSpec((tm, d), lambda i: (i, 0))
    return pl.pallas_call(
        functools.partial(_odd_kernel, tiles_per_seq=seq_len // tm),
        grid=(t // tm,),
        in_specs=[row, _resident((1, d)), _resident((d, 3 * CONV_DIM)),
                  _resident((CONV_WIDTH, CONV_DIM)), _resident((CONV_DIM, d))],
        out_specs=row,
        out_shape=jax.ShapeDtypeStruct((t, d), F32),
        scratch_shapes=[pltpu.VMEM((CONV_HALO + tm, CONV_DIM), F32)],
        compiler_params=pltpu.CompilerParams(
            dimension_semantics=("arbitrary",),
            vmem_limit_bytes=_vmem_limit(4 * d * CONV_DIM * 2, 4 * tm * d * 4,
                                         2 * tm * 3 * CONV_DIM * 4)),
        name="odd_mixer",
    )(x, gain, w_in, conv_w, w_out)


def kernel(x, ffn1_norm, ffn1_w_gate, ffn1_w_up, ffn1_w_down, ffn2_norm, ffn2_w_gate,
           ffn2_w_up, ffn2_w_down, mix_norm, even_w_in, even_w_out, q_gain, k_gain,
           subln_gain, lambda_q1, lambda_k1, lambda_q2, lambda_k2, w_pool, pool_scale,
           odd_w_in, conv_w, odd_w_out):
    batch, seq, d = x.shape
    assert d == D_MODEL and seq % TOKEN_TILE == 0 and seq % Q_TILE == 0
    assert TOKEN_TILE >= 2 * POOL_HALO
    bf = lambda w: w.astype(BF16)
    vec = lambda v: v.reshape(1, -1).astype(F32)
    xt = x.reshape(batch * seq, d)
    for l in range(DEPTH):
        i = l // 2
        xt = _ffn(xt, vec(ffn1_norm[l]), bf(ffn1_w_gate[l]), bf(ffn1_w_up[l]),
                  bf(ffn1_w_down[l]))
        if l % 2 == 0:
            lam_init = 0.8 - 0.6 * math.exp(-0.3 * l)
            reps = ATTN_WIDTH // DIFF_HEAD_DIM
            qk_gain = jnp.concatenate([jnp.tile(q_gain[i], reps) * DIFF_HEAD_DIM ** -0.5,
                                       jnp.tile(k_gain[i], reps)])
            qkv, p = _even_in(xt, vec(mix_norm[l]), bf(even_w_in[i]), vec(qk_gain),
                              bf(w_pool[i]), vec(pool_scale[i]), seq)
            o = _attention(qkv.reshape(batch, seq, 3 * ATTN_WIDTH), vec(lambda_q1[i]),
                           vec(lambda_k1[i]), vec(lambda_q2[i]), vec(lambda_k2[i]),
                           vec(subln_gain[i]), lam_init)
            xt = _even_out(xt, o.reshape(batch * seq, ATTN_WIDTH), p, bf(even_w_out[i]))
        else:
            xt = _odd_mixer(xt, vec(mix_norm[l]), bf(odd_w_in[i]), conv_w[i].astype(F32),
                            bf(odd_w_out[i]), seq)
        xt = _ffn(xt, vec(ffn2_norm[l]), bf(ffn2_w_gate[l]), bf(ffn2_w_up[l]),
                  bf(ffn2_w_down[l]))
    return xt.reshape(batch, seq, d)
```

```python
import functools
import math

import jax
import jax.numpy as jnp
from jax import lax
from jax.experimental import pallas as pl
from jax.experimental.pallas import tpu as pltpu

D_MODEL = 1024
D_FF = 2816
DEPTH = 2
ATTN_WIDTH = 512
DIFF_HEADS = 4
DIFF_HEAD_DIM = 64
DIFF_V_DIM = 2 * DIFF_HEAD_DIM
POOL_WIDTH = 512
POOL_GROUPS = 4
POOL_GROUP_DIM = POOL_WIDTH // POOL_GROUPS
POOL_WINDOWS = (2, 4, 8, 16)
EVEN_IN = 3 * ATTN_WIDTH + POOL_WIDTH
CONV_WIDTH = 3
CONV_DIM = D_MODEL
EPS = 1e-6

LANES = 128
SUBLANES = 8
MIB = 1024 * 1024

TOKEN_TILE = 512
FF_CHUNK = 256
Q_TILE = 256
ATTN_HEADS_PER_STEP = 2
POOL_HALO = 16
CONV_HALO = SUBLANES

F32 = jnp.float32
BF16 = jnp.bfloat16


def _vmem_limit(*nbytes):
    total = sum(nbytes)
    return int(min(60 * MIB, (total * 5 // 4 // MIB + 4) * MIB))


def _resident(shape):
    zeros = (0,) * len(shape)
    return pl.BlockSpec(shape, lambda *_: zeros, pipeline_mode=pl.Buffered(1))


def _resident_layer(shape, layer):
    index = (layer,) + (0,) * len(shape)
    return pl.BlockSpec((None,) + shape, lambda *_: index, pipeline_mode=pl.Buffered(1))


def _rms_scale(x):
    return lax.rsqrt(jnp.mean(x * x, axis=-1, keepdims=True) + EPS)


def _dot(a, b):
    return jnp.dot(a, b, preferred_element_type=F32)


def _ffn_kernel(*refs, mix_in):
    if mix_in:
        x_ref, a_ref, p_ref, wo_ref, g_ref, wg_ref, wu_ref, wd_ref, o_ref = refs
        x = (x_ref[...] + _dot(a_ref[...], wo_ref[0:ATTN_WIDTH, :].astype(BF16))
             + _dot(p_ref[...], wo_ref[ATTN_WIDTH:, :].astype(BF16)))
    else:
        x_ref, g_ref, wg_ref, wu_ref, wd_ref, o_ref = refs
        x = x_ref[...]
    xn = (x * _rms_scale(x) * g_ref[...]).astype(BF16)
    acc = jnp.zeros(x.shape, F32)
    for c in range(D_FF // FF_CHUNK):
        sl = slice(c * FF_CHUNK, (c + 1) * FF_CHUNK)
        hg = _dot(xn, wg_ref[:, sl].astype(BF16))
        hu = _dot(xn, wu_ref[:, sl].astype(BF16))
        a = (hg * jax.nn.sigmoid(hg) * hu).astype(BF16)
        acc = acc + _dot(a, wd_ref[sl, :].astype(BF16))
    o_ref[...] = x + 0.5 * acc


def _ffn(x, gain, w_gate, w_up, w_down, layer, mix=None):
    t, d = x.shape
    tm = TOKEN_TILE
    row = lambda w: pl.BlockSpec((tm, w), lambda i: (i, 0))
    ffn_specs = [_resident_layer((1, d), layer), _resident_layer((d, D_FF), layer),
                 _resident_layer((d, D_FF), layer), _resident_layer((D_FF, d), layer)]
    ffn_args = (gain.reshape(DEPTH, 1, d), w_gate, w_up, w_down)
    if mix is None:
        mix_specs, mix_args, mix_bytes = [], (), 0
    else:
        attn, pool, w_out, mix_layer = mix
        mix_specs = [row(ATTN_WIDTH), row(POOL_WIDTH), _resident_layer((d, d), mix_layer)]
        mix_args = (attn, pool, w_out)
        mix_bytes = d * d * 4 + 2 * tm * d * 2
    return pl.pallas_call(
        functools.partial(_ffn_kernel, mix_in=mix is not None),
        grid=(t // tm,),
        in_specs=[row(d)] + mix_specs + ffn_specs,
        out_specs=row(d),
        out_shape=jax.ShapeDtypeStruct((t, d), F32),
        compiler_params=pltpu.CompilerParams(
            dimension_semantics=("parallel",),
            vmem_limit_bytes=_vmem_limit(3 * d * D_FF * 4, 4 * tm * d * 4, mix_bytes,
                                         2 * tm * d * 4, 3 * tm * FF_CHUNK * 4)),
        name="ffn_mix" if mix is not None else "ffn",
    )(x, *mix_args, *ffn_args)


def _even_in_kernel(x_ref, g_ref, w_ref, qkg_ref, wp_ref, ps_ref, qkv_ref, p_ref,
                    ubuf, *, tiles_per_seq):
    tm = x_ref.shape[0]
    x = x_ref[...]
    h = (x * _rms_scale(x) * g_ref[...]).astype(BF16)
    proj = _dot(h, w_ref[...].astype(BF16))

    lo = lax.broadcasted_iota(jnp.int32, (1, LANES), 1) < DIFF_HEAD_DIM
    for j in range(2 * ATTN_WIDTH // LANES):
        sl = slice(j * LANES, (j + 1) * LANES)
        blk = proj[:, sl]
        sq = blk * blk
        ss_lo = jnp.sum(jnp.where(lo, sq, 0.0), axis=-1, keepdims=True)
        ss_hi = jnp.sum(jnp.where(lo, 0.0, sq), axis=-1, keepdims=True)
        ms = jnp.where(lo, ss_lo, ss_hi) * (1.0 / DIFF_HEAD_DIM)
        qkv_ref[:, sl] = (blk * lax.rsqrt(ms + EPS) * qkg_ref[:, sl]).astype(BF16)
    v_sl = slice(2 * ATTN_WIDTH, 3 * ATTN_WIDTH)
    qkv_ref[:, v_sl] = proj[:, v_sl].astype(BF16)

    seq_tile = pl.program_id(0) % tiles_per_seq

    @pl.when(seq_tile == 0)
    def _():
        ubuf[0:POOL_HALO, :] = jnp.zeros((POOL_HALO, POOL_WIDTH), F32)

    ubuf[POOL_HALO:POOL_HALO + tm, :] = proj[:, 3 * ATTN_WIDTH:]
    pos = seq_tile * tm + lax.broadcasted_iota(jnp.int32, (tm, 1), 0)
    for g, win in enumerate(POOL_WINDOWS):
        sl = slice(g * POOL_GROUP_DIM, (g + 1) * POOL_GROUP_DIM)
        u = ubuf[POOL_HALO:POOL_HALO + tm, sl]
        acc = u
        for j in range(1, win):
            acc = acc + ubuf[POOL_HALO - j:POOL_HALO - j + tm, sl]
        count = jnp.minimum(pos + 1, win).astype(F32)
        pooled = (acc / count - u).astype(BF16)
        y = _dot(pooled, wp_ref[g].astype(BF16))
        p_ref[:, sl] = (y * ps_ref[:, sl]).astype(BF16)
    ubuf[0:POOL_HALO, :] = ubuf[tm:tm + POOL_HALO, :]


def _even_in(x, gain, w_in, qk_gain, w_pool, pool_scale, seq_len, layer):
    t, d = x.shape
    tm = TOKEN_TILE
    row = lambda w: pl.BlockSpec((tm, w), lambda i: (i, 0))
    return pl.pallas_call(
        functools.partial(_even_in_kernel, tiles_per_seq=seq_len // tm),
        grid=(t // tm,),
        in_specs=[row(d), _resident((1, d)), _resident_layer((d, EVEN_IN), layer),
                  _resident((1, 2 * ATTN_WIDTH)),
                  _resident_layer((POOL_GROUPS, POOL_GROUP_DIM, POOL_GROUP_DIM), layer),
                  _resident((1, POOL_WIDTH))],
        out_specs=[row(3 * ATTN_WIDTH), row(POOL_WIDTH)],
        out_shape=[jax.ShapeDtypeStruct((t, 3 * ATTN_WIDTH), BF16),
                   jax.ShapeDtypeStruct((t, POOL_WIDTH), BF16)],
        scratch_shapes=[pltpu.VMEM((POOL_HALO + tm, POOL_WIDTH), F32)],
        compiler_params=pltpu.CompilerParams(
            dimension_semantics=("arbitrary",),
            vmem_limit_bytes=_vmem_limit(d * EVEN_IN * 6, 2 * tm * d * 4,
                                         2 * tm * EVEN_IN * 2, 2 * tm * EVEN_IN * 4)),
        name="even_in",
    )(x, gain, w_in, qk_gain, w_pool, pool_scale)


def _attn_kernel(lq1_ref, lk1_ref, lq2_ref, lk2_ref, sg_ref, q_ref, k_ref, v_ref, o_ref,
                 vext, *, lam_init):
    seq = q_ref.shape[0]
    heads = q_ref.shape[1] // LANES
    tq = Q_TILE
    n_blocks = seq // tq
    vw = DIFF_V_DIM + LANES
    lam = (jnp.exp(jnp.sum(lq1_ref[...] * lk1_ref[...], axis=-1, keepdims=True))
           - jnp.exp(jnp.sum(lq2_ref[...] * lk2_ref[...], axis=-1, keepdims=True))
           + lam_init)
    lo = lax.broadcasted_iota(jnp.int32, (1, LANES), 1) < DIFF_HEAD_DIM
    out_gain = sg_ref[...] * (1.0 - lam_init)
    contract_last = (((1,), (1,)), ((), ()))
    for h in range(heads):
        vext[:, h * vw:h * vw + DIFF_V_DIM] = v_ref[:, h * LANES:(h + 1) * LANES]
        vext[:, h * vw + DIFF_V_DIM:(h + 1) * vw] = jnp.ones((seq, LANES), BF16)

    rows = lax.broadcasted_iota(jnp.int32, (2 * tq, 1), 0)
    rows = jnp.where(rows >= tq, rows - tq, rows)
    causal = rows >= lax.broadcasted_iota(jnp.int32, (1, tq), 1)

    def block(h, i):
        past = i * tq
        hs = slice(h * LANES, (h + 1) * LANES)
        vs = slice(h * vw, (h + 1) * vw)
        q = q_ref[past:past + tq, hs]
        zero = jnp.zeros_like(q)
        qq = jnp.concatenate([jnp.where(lo, q, zero), jnp.where(lo, zero, q)], axis=0)
        s_d = lax.dot_general(qq, k_ref[past:past + tq, hs], contract_last,
                              preferred_element_type=F32)
        s_d = jnp.where(causal, s_d, -jnp.inf)
        m = jnp.max(s_d, axis=-1, keepdims=True)
        if i > 0:
            s_p = lax.dot_general(qq, k_ref[0:past, hs], contract_last,
                                  preferred_element_type=F32)
            m = jnp.maximum(m, jnp.max(s_p, axis=-1, keepdims=True))
        oe = _dot(jnp.exp2(s_d - m).astype(BF16), vext[past:past + tq, vs])
        if i > 0:
            oe = oe + _dot(jnp.exp2(s_p - m).astype(BF16), vext[0:past, vs])
        r = oe[:, 0:DIFF_V_DIM] / oe[:, DIFF_V_DIM:]
        o = r[0:tq, :] - lam * r[tq:, :]
        o_ref[past:past + tq, hs] = (o * _rms_scale(o) * out_gain).astype(BF16)

    for i in range(n_blocks):
        for h in range(heads):
            block(h, i if h % 2 == 0 else n_blocks - 1 - i)


def _attention(qkv, lq1, lk1, lq2, lk2, subln_gain, lam_init):
    b, s, _ = qkv.shape
    hps = ATTN_HEADS_PER_STEP
    width = hps * LANES
    groups = DIFF_HEADS // hps
    head = lambda off: pl.BlockSpec((None, s, width), lambda bi, gi: (bi, 0, off + gi))
    vec = lambda n: pl.BlockSpec((1, n), lambda bi, gi: (0, 0))
    return pl.pallas_call(
        functools.partial(_attn_kernel, lam_init=lam_init),
        grid=(b, groups),
        in_specs=[vec(DIFF_HEAD_DIM)] * 4 + [vec(DIFF_V_DIM), head(0), head(groups),
                                             head(2 * groups)],
        out_specs=pl.BlockSpec((None, s, width), lambda bi, gi: (bi, 0, gi)),
        out_shape=jax.ShapeDtypeStruct((b, s, ATTN_WIDTH), BF16),
        scratch_shapes=[pltpu.VMEM((s, hps * (DIFF_V_DIM + LANES)), BF16)],
        compiler_params=pltpu.CompilerParams(
            dimension_semantics=("parallel", "parallel"),
            vmem_limit_bytes=_vmem_limit(10 * s * width * 2, 8 * hps * Q_TILE * s * 4)),
        name="diff_attn",
    )(lq1, lk1, lq2, lk2, subln_gain, qkv, qkv, qkv)


def _odd_kernel(x_ref, g_ref, wi_ref, cw_ref, wo_ref, y_ref, zbuf, *, tiles_per_seq):
    tm = x_ref.shape[0]
    x = x_ref[...]
    h = (x * _rms_scale(x) * g_ref[...]).astype(BF16)
    bch = _dot(h, wi_ref[...].astype(BF16))
    b_gate = bch[:, 0:CONV_DIM]
    z = bch[:, CONV_DIM:2 * CONV_DIM] * bch[:, 2 * CONV_DIM:]

    @pl.when(pl.program_id(0) % tiles_per_seq == 0)
    def _():
        zbuf[0:CONV_HALO, :] = jnp.zeros((CONV_HALO, CONV_DIM), F32)

    zbuf[CONV_HALO:CONV_HALO + tm, :] = z
    conv = cw_ref[CONV_WIDTH - 1:CONV_WIDTH, :] * z
    for j in range(1, CONV_WIDTH):
        tap = cw_ref[CONV_WIDTH - 1 - j:CONV_WIDTH - j, :]
        conv = conv + tap * zbuf[CONV_HALO - j:CONV_HALO - j + tm, :]
    zbuf[0:CONV_HALO, :] = zbuf[tm:tm + CONV_HALO, :]
    y_ref[...] = x + _dot((b_gate * conv).astype(BF16), wo_ref[...].astype(BF16))


def _odd_mixer(x, gain, w_in, conv_w, w_out, seq_len, layer):
    t, d = x.shape
    tm = TOKEN_TILE
    row = pl.BlockSpec((tm, d), lambda i: (i, 0))
    return pl.pallas_call(
        functools.partial(_odd_kernel, tiles_per_seq=seq_len // tm),
        grid=(t // tm,),
        in_specs=[row, _resident((1, d)), _resident_layer((d, 3 * CONV_DIM), layer),
                  _resident_layer((CONV_WIDTH, CONV_DIM), layer),
                  _resident_layer((CONV_DIM, d), layer)],
        out_specs=row,
        out_shape=jax.ShapeDtypeStruct((t, d), F32),
        scratch_shapes=[pltpu.VMEM((CONV_HALO + tm, CONV_DIM), F32)],
        compiler_params=pltpu.CompilerParams(
            dimension_semantics=("arbitrary",),
            vmem_limit_bytes=_vmem_limit(4 * d * CONV_DIM * 4, 4 * tm * d * 4,
                                         2 * tm * 3 * CONV_DIM * 4)),
        name="odd_mixer",
    )(x, gain, w_in, conv_w, w_out)


def kernel(x, ffn1_norm, ffn1_w_gate, ffn1_w_up, ffn1_w_down, ffn2_norm, ffn2_w_gate,
           ffn2_w_up, ffn2_w_down, mix_norm, even_w_in, even_w_out, q_gain, k_gain,
           subln_gain, lambda_q1, lambda_k1, lambda_q2, lambda_k2, w_pool, pool_scale,
           odd_w_in, conv_w, odd_w_out):
    batch, seq, d = x.shape
    assert d == D_MODEL and seq % TOKEN_TILE == 0 and seq % Q_TILE == 0
    assert TOKEN_TILE >= 2 * POOL_HALO
    vec = lambda v: v.reshape(1, -1).astype(F32)
    xt = x.reshape(batch * seq, d)
    for l in range(DEPTH):
        i = l // 2
        xt = _ffn(xt, ffn1_norm, ffn1_w_gate, ffn1_w_up, ffn1_w_down, l)
        if l % 2 == 0:
            lam_init = 0.8 - 0.6 * math.exp(-0.3 * l)
            reps = ATTN_WIDTH // DIFF_HEAD_DIM
            q_scale = math.log2(math.e) * DIFF_HEAD_DIM ** -0.5
            qk_gain = jnp.concatenate([jnp.tile(q_gain[i], reps) * q_scale,
                                       jnp.tile(k_gain[i], reps)])
            qkv, p = _even_in(xt, vec(mix_norm[l]), even_w_in, vec(qk_gain), w_pool,
                              vec(pool_scale[i]), seq, i)
            o = _attention(qkv.reshape(batch, seq, 3 * ATTN_WIDTH), vec(lambda_q1[i]),
                           vec(lambda_k1[i]), vec(lambda_q2[i]), vec(lambda_k2[i]),
                           vec(subln_gain[i]), lam_init)
            mix = (o.reshape(batch * seq, ATTN_WIDTH), p, even_w_out, i)
        else:
            xt = _odd_mixer(xt, vec(mix_norm[l]), odd_w_in, conv_w, odd_w_out, seq, i)
            mix = None
        xt = _ffn(xt, ffn2_norm, ffn2_w_gate, ffn2_w_up, ffn2_w_down, l, mix)
    return xt.reshape(batch, seq, d)
```

```python
import functools
import math

import jax
import jax.numpy as jnp
from jax import lax
from jax.experimental import pallas as pl
from jax.experimental.pallas import tpu as pltpu

D_MODEL = 1024
D_FF = 2816
DEPTH = 2
ATTN_WIDTH = 512
DIFF_HEADS = 4
DIFF_HEAD_DIM = 64
DIFF_V_DIM = 2 * DIFF_HEAD_DIM
POOL_WIDTH = 512
POOL_GROUPS = 4
POOL_GROUP_DIM = POOL_WIDTH // POOL_GROUPS
POOL_WINDOWS = (2, 4, 8, 16)
EVEN_IN = 3 * ATTN_WIDTH + POOL_WIDTH
CONV_WIDTH = 3
CONV_DIM = D_MODEL
EPS = 1e-6

LANES = 128
SUBLANES = 8
MIB = 1024 * 1024

TOKEN_TILE = 512
FF_CHUNK = 256
Q_TILE = 256
ATTN_HEADS_PER_STEP = 2
POOL_HALO = max(POOL_WINDOWS)
POOL_PAD = SUBLANES
CONV_HALO = SUBLANES
CONV_CHUNK = 1024

F32 = jnp.float32
BF16 = jnp.bfloat16


def _vmem_limit(*nbytes):
    total = sum(nbytes)
    return int(min(60 * MIB, (total * 5 // 4 // MIB + 4) * MIB))


def _resident(shape):
    zeros = (0,) * len(shape)
    return pl.BlockSpec(shape, lambda *_: zeros, pipeline_mode=pl.Buffered(1))


def _resident_layer(shape, layer):
    index = (layer,) + (0,) * len(shape)
    return pl.BlockSpec((None,) + shape, lambda *_: index, pipeline_mode=pl.Buffered(1))


def _rms_scale(x):
    return lax.rsqrt(jnp.mean(x * x, axis=-1, keepdims=True) + EPS)


def _dot(a, b):
    return jnp.dot(a, b, preferred_element_type=F32)


def _ffn_kernel(*refs, mix_in):
    if mix_in:
        x_ref, a_ref, p_ref, wo_ref, g_ref, wg_ref, wu_ref, wd_ref, o_ref = refs
        x = (x_ref[...] + _dot(a_ref[...], wo_ref[0:ATTN_WIDTH, :].astype(BF16))
             + _dot(p_ref[...], wo_ref[ATTN_WIDTH:, :].astype(BF16)))
    else:
        x_ref, g_ref, wg_ref, wu_ref, wd_ref, o_ref = refs
        x = x_ref[...]
    xn = (x * _rms_scale(x) * g_ref[...]).astype(BF16)
    acc = jnp.zeros(x.shape, F32)
    for c in range(D_FF // FF_CHUNK):
        sl = slice(c * FF_CHUNK, (c + 1) * FF_CHUNK)
        hg = _dot(xn, wg_ref[:, sl].astype(BF16))
        hu = _dot(xn, wu_ref[:, sl].astype(BF16))
        a = (hg * jax.nn.sigmoid(hg) * hu).astype(BF16)
        acc = acc + _dot(a, wd_ref[sl, :].astype(BF16))
    o_ref[...] = x + 0.5 * acc


def _ffn(x, gain, w_gate, w_up, w_down, layer, mix=None):
    t, d = x.shape
    tm = TOKEN_TILE
    row = lambda w: pl.BlockSpec((tm, w), lambda i: (i, 0))
    ffn_specs = [_resident_layer((1, d), layer), _resident_layer((d, D_FF), layer),
                 _resident_layer((d, D_FF), layer), _resident_layer((D_FF, d), layer)]
    ffn_args = (gain.reshape(DEPTH, 1, d), w_gate, w_up, w_down)
    if mix is None:
        mix_specs, mix_args, mix_bytes = [], (), 0
    else:
        attn, pool, w_out, mix_layer = mix
        mix_specs = [row(ATTN_WIDTH), row(POOL_WIDTH), _resident_layer((d, d), mix_layer)]
        mix_args = (attn, pool, w_out)
        mix_bytes = d * d * 4 + 2 * tm * d * 2
    return pl.pallas_call(
        functools.partial(_ffn_kernel, mix_in=mix is not None),
        grid=(t // tm,),
        in_specs=[row(d)] + mix_specs + ffn_specs,
        out_specs=row(d),
        out_shape=jax.ShapeDtypeStruct((t, d), F32),
        compiler_params=pltpu.CompilerParams(
            dimension_semantics=("parallel",),
            vmem_limit_bytes=_vmem_limit(3 * d * D_FF * 4, 4 * tm * d * 4, mix_bytes,
                                         2 * tm * d * 4, 3 * tm * FF_CHUNK * 4)),
        name="ffn_mix" if mix is not None else "ffn",
    )(x, *mix_args, *ffn_args)


def _even_in_kernel(x_ref, g_ref, w_ref, qkg_ref, wp_ref, ps_ref, qkv_ref, p_ref,
                    ubuf, *, tiles_per_seq):
    tm = x_ref.shape[0]
    lead = POOL_PAD + POOL_HALO

    @pl.when(pl.program_id(0) == 0)
    def _():
        ubuf[0:lead, :] = jnp.zeros((lead, POOL_WIDTH), F32)

    x = x_ref[...]
    h = (x * _rms_scale(x) * g_ref[...]).astype(BF16)
    proj = _dot(h, w_ref[...].astype(BF16))

    lo = lax.broadcasted_iota(jnp.int32, (1, LANES), 1) < DIFF_HEAD_DIM
    for j in range(2 * ATTN_WIDTH // LANES):
        sl = slice(j * LANES, (j + 1) * LANES)
        blk = proj[:, sl]
        sq = blk * blk
        ss_lo = jnp.sum(jnp.where(lo, sq, 0.0), axis=-1, keepdims=True)
        ss_hi = jnp.sum(jnp.where(lo, 0.0, sq), axis=-1, keepdims=True)
        ms = jnp.where(lo, ss_lo, ss_hi) * (1.0 / DIFF_HEAD_DIM)
        qkv_ref[:, sl] = (blk * lax.rsqrt(ms + EPS) * qkg_ref[:, sl]).astype(BF16)
    v_sl = slice(2 * ATTN_WIDTH, 3 * ATTN_WIDTH)
    qkv_ref[:, v_sl] = proj[:, v_sl].astype(BF16)

    seq_tile = pl.program_id(0) % tiles_per_seq
    u_all = proj[:, 3 * ATTN_WIDTH:]
    ubuf[POOL_PAD:lead, :] = jnp.where(seq_tile == 0, 0.0, ubuf[POOL_PAD:lead, :])
    ubuf[lead:lead + tm, :] = u_all
    pos = seq_tile * tm + lax.broadcasted_iota(jnp.int32, (tm, 1), 0)
    for g, win in enumerate(POOL_WINDOWS):
        sl = slice(g * POOL_GROUP_DIM, (g + 1) * POOL_GROUP_DIM)
        w = 1
        while w < min(win, SUBLANES):
            ubuf[POOL_PAD:lead + tm, sl] = (ubuf[POOL_PAD:lead + tm, sl]
                                            + ubuf[POOL_PAD - w:lead + tm - w, sl])
            w *= 2
        acc = ubuf[lead:lead + tm, sl]
        while w < win:
            acc = acc + ubuf[lead - w:lead - w + tm, sl]
            w *= 2
        count = jnp.minimum(pos + 1, win).astype(F32)
        pooled = (acc / count - u_all[:, sl]).astype(BF16)
        y = _dot(pooled, wp_ref[g].astype(BF16))
        p_ref[:, sl] = (y * ps_ref[:, sl]).astype(BF16)
    ubuf[POOL_PAD:lead, :] = u_all[tm - POOL_HALO:, :]


def _even_in(x, gain, w_in, qk_gain, w_pool, pool_scale, seq_len, layer):
    t, d = x.shape
    tm = TOKEN_TILE
    row = lambda w: pl.BlockSpec((tm, w), lambda i: (i, 0))
    return pl.pallas_call(
        functools.partial(_even_in_kernel, tiles_per_seq=seq_len // tm),
        grid=(t // tm,),
        in_specs=[row(d), _resident((1, d)), _resident_layer((d, EVEN_IN), layer),
                  _resident((1, 2 * ATTN_WIDTH)),
                  _resident_layer((POOL_GROUPS, POOL_GROUP_DIM, POOL_GROUP_DIM), layer),
                  _resident((1, POOL_WIDTH))],
        out_specs=[row(3 * ATTN_WIDTH), row(POOL_WIDTH)],
        out_shape=[jax.ShapeDtypeStruct((t, 3 * ATTN_WIDTH), BF16),
                   jax.ShapeDtypeStruct((t, POOL_WIDTH), BF16)],
        scratch_shapes=[pltpu.VMEM((POOL_PAD + POOL_HALO + tm, POOL_WIDTH), F32)],
        compiler_params=pltpu.CompilerParams(
            dimension_semantics=("arbitrary",),
            vmem_limit_bytes=_vmem_limit(d * EVEN_IN * 6, 2 * tm * d * 4,
                                         2 * tm * EVEN_IN * 2, 2 * tm * EVEN_IN * 4)),
        name="even_in",
    )(x, gain, w_in, qk_gain, w_pool, pool_scale)


def _attn_kernel(lq1_ref, lk1_ref, lq2_ref, lk2_ref, sg_ref, q_ref, k_ref, v_ref, o_ref,
                 vext, *, lam_init):
    seq = q_ref.shape[0]
    heads = q_ref.shape[1] // LANES
    tq = Q_TILE
    n_blocks = seq // tq
    vw = DIFF_V_DIM + LANES
    lam = (jnp.exp(jnp.sum(lq1_ref[...] * lk1_ref[...], axis=-1, keepdims=True))
           - jnp.exp(jnp.sum(lq2_ref[...] * lk2_ref[...], axis=-1, keepdims=True))
           + lam_init)
    lo = lax.broadcasted_iota(jnp.int32, (1, LANES), 1) < DIFF_HEAD_DIM
    out_gain = sg_ref[...] * (1.0 - lam_init)
    contract_last = (((1,), (1,)), ((), ()))
    for h in range(heads):
        vext[:, h * vw:h * vw + DIFF_V_DIM] = v_ref[:, h * LANES:(h + 1) * LANES]
        vext[:, h * vw + DIFF_V_DIM:(h + 1) * vw] = jnp.ones((seq, LANES), BF16)

    rows = lax.broadcasted_iota(jnp.int32, (2 * tq, 1), 0)
    rows = jnp.where(rows >= tq, rows - tq, rows)
    causal = rows >= lax.broadcasted_iota(jnp.int32, (1, tq), 1)

    def block(h, i):
        past = i * tq
        hs = slice(h * LANES, (h + 1) * LANES)
        vs = slice(h * vw, (h + 1) * vw)
        q = q_ref[past:past + tq, hs]
        zero = jnp.zeros_like(q)
        qq = jnp.concatenate([jnp.where(lo, q, zero), jnp.where(lo, zero, q)], axis=0)
        s_d = lax.dot_general(qq, k_ref[past:past + tq, hs], contract_last,
                              preferred_element_type=F32)
        s_d = jnp.where(causal, s_d, -jnp.inf)
        m = jnp.max(s_d, axis=-1, keepdims=True)
        if i > 0:
            s_p = lax.dot_general(qq, k_ref[0:past, hs], contract_last,
                                  preferred_element_type=F32)
            m = jnp.maximum(m, jnp.max(s_p, axis=-1, keepdims=True))
        oe = _dot(jnp.exp2(s_d - m).astype(BF16), vext[past:past + tq, vs])
        if i > 0:
            oe = oe + _dot(jnp.exp2(s_p - m).astype(BF16), vext[0:past, vs])
        r = oe[:, 0:DIFF_V_DIM] / oe[:, DIFF_V_DIM:]
        o = r[0:tq, :] - lam * r[tq:, :]
        o_ref[past:past + tq, hs] = (o * _rms_scale(o) * out_gain).astype(BF16)

    for i in range(n_blocks):
        for h in range(heads):
            block(h, i if h % 2 == 0 else n_blocks - 1 - i)


def _attention(qkv, lq1, lk1, lq2, lk2, subln_gain, lam_init):
    b, s, _ = qkv.shape
    hps = ATTN_HEADS_PER_STEP
    width = hps * LANES
    groups = DIFF_HEADS // hps
    head = lambda off: pl.BlockSpec((None, s, width), lambda bi, gi: (bi, 0, off + gi))
    vec = lambda n: pl.BlockSpec((1, n), lambda bi, gi: (0, 0))
    return pl.pallas_call(
        functools.partial(_attn_kernel, lam_init=lam_init),
        grid=(b, groups),
        in_specs=[vec(DIFF_HEAD_DIM)] * 4 + [vec(DIFF_V_DIM), head(0), head(groups),
                                             head(2 * groups)],
        out_specs=pl.BlockSpec((None, s, width), lambda bi, gi: (bi, 0, gi)),
        out_shape=jax.ShapeDtypeStruct((b, s, ATTN_WIDTH), BF16),
        scratch_shapes=[pltpu.VMEM((s, hps * (DIFF_V_DIM + LANES)), BF16)],
        compiler_params=pltpu.CompilerParams(
            dimension_semantics=("parallel", "parallel"),
            vmem_limit_bytes=_vmem_limit(10 * s * width * 2, 8 * hps * Q_TILE * s * 4)),
        name="diff_attn",
    )(lq1, lk1, lq2, lk2, subln_gain, qkv, qkv, qkv)


def _odd_kernel(x_ref, g_ref, wi_ref, cw_ref, wo_ref, y_ref, zbuf, *, tiles_per_seq):
    tm = x_ref.shape[0]

    @pl.when(pl.program_id(0) == 0)
    def _():
        zbuf[0:CONV_HALO, :] = jnp.zeros((CONV_HALO, CONV_DIM), F32)

    x = x_ref[...]
    h = (x * _rms_scale(x) * g_ref[...]).astype(BF16)
    seq_start = pl.program_id(0) % tiles_per_seq == 0
    acc = jnp.zeros(x.shape, F32)
    for c in range(CONV_DIM // CONV_CHUNK):
        sl = slice(c * CONV_CHUNK, (c + 1) * CONV_CHUNK)
        proj = [_dot(h, wi_ref[:, part * CONV_DIM + c * CONV_CHUNK:
                               part * CONV_DIM + (c + 1) * CONV_CHUNK].astype(BF16))
                for part in range(3)]
        z = proj[1] * proj[2]
        zbuf[0:CONV_HALO, sl] = jnp.where(seq_start, 0.0, zbuf[0:CONV_HALO, sl])
        zbuf[CONV_HALO:CONV_HALO + tm, sl] = z
        conv = cw_ref[CONV_WIDTH - 1:CONV_WIDTH, sl] * z
        for j in range(1, CONV_WIDTH):
            tap = cw_ref[CONV_WIDTH - 1 - j:CONV_WIDTH - j, sl]
            conv = conv + tap * zbuf[CONV_HALO - j:CONV_HALO - j + tm, sl]
        zbuf[0:CONV_HALO, sl] = z[tm - CONV_HALO:, :]
        acc = acc + _dot((proj[0] * conv).astype(BF16), wo_ref[sl, :].astype(BF16))
    y_ref[...] = x + acc


def _odd_mixer(x, gain, w_in, conv_w, w_out, seq_len, layer):
    t, d = x.shape
    tm = TOKEN_TILE
    row = pl.BlockSpec((tm, d), lambda i: (i, 0))
    return pl.pallas_call(
        functools.partial(_odd_kernel, tiles_per_seq=seq_len // tm),
        grid=(t // tm,),
        in_specs=[row, _resident((1, d)), _resident_layer((d, 3 * CONV_DIM), layer),
                  _resident_layer((CONV_WIDTH, CONV_DIM), layer),
                  _resident_layer((CONV_DIM, d), layer)],
        out_specs=row,
        out_shape=jax.ShapeDtypeStruct((t, d), F32),
        scratch_shapes=[pltpu.VMEM((CONV_HALO + tm, CONV_DIM), F32)],
        compiler_params=pltpu.CompilerParams(
            dimension_semantics=("arbitrary",),
            vmem_limit_bytes=_vmem_limit(4 * d * CONV_DIM * 4, 4 * tm * d * 4,
                                         3 * tm * d * 4, 6 * tm * CONV_CHUNK * 4)),
        name="odd_mixer",
    )(x, gain, w_in, conv_w, w_out)


def kernel(x, ffn1_norm, ffn1_w_gate, ffn1_w_up, ffn1_w_down, ffn2_norm, ffn2_w_gate,
           ffn2_w_up, ffn2_w_down, mix_norm, even_w_in, even_w_out, q_gain, k_gain,
           subln_gain, lambda_q1, lambda_k1, lambda_q2, lambda_k2, w_pool, pool_scale,
           odd_w_in, conv_w, odd_w_out):
    batch, seq, d = x.shape
    assert d == D_MODEL and seq % TOKEN_TILE == 0 and seq % Q_TILE == 0
    assert TOKEN_TILE >= 2 * POOL_HALO
    vec = lambda v: v.reshape(1, -1).astype(F32)
    xt = x.reshape(batch * seq, d)
    for l in range(DEPTH):
        i = l // 2
        xt = _ffn(xt, ffn1_norm, ffn1_w_gate, ffn1_w_up, ffn1_w_down, l)
        if l % 2 == 0:
            lam_init = 0.8 - 0.6 * math.exp(-0.3 * l)
            reps = ATTN_WIDTH // DIFF_HEAD_DIM
            q_scale = math.log2(math.e) * DIFF_HEAD_DIM ** -0.5
            qk_gain = jnp.concatenate([jnp.tile(q_gain[i], reps) * q_scale,
                                       jnp.tile(k_gain[i], reps)])
            qkv, p = _even_in(xt, vec(mix_norm[l]), even_w_in, vec(qk_gain), w_pool,
                              vec(pool_scale[i]), seq, i)
            o = _attention(qkv.reshape(batch, seq, 3 * ATTN_WIDTH), vec(lambda_q1[i]),
                           vec(lambda_k1[i]), vec(lambda_q2[i]), vec(lambda_k2[i]),
                           vec(subln_gain[i]), lam_init)
            mix = (o.reshape(batch * seq, ATTN_WIDTH), p, even_w_out, i)
        else:
            xt = _odd_mixer(xt, vec(mix_norm[l]), odd_w_in, conv_w, odd_w_out, seq, i)
            mix = None
        xt = _ffn(xt, ffn2_norm, ffn2_w_gate, ffn2_w_up, ffn2_w_down, l, mix)
    return xt.reshape(batch, seq, d)
```

```python
import functools
import math

import jax
import jax.numpy as jnp
from jax import lax
from jax.experimental import pallas as pl
from jax.experimental.pallas import tpu as pltpu

D_MODEL = 1024
D_FF = 2816
DEPTH = 2
ATTN_WIDTH = 512
DIFF_HEADS = 4
DIFF_HEAD_DIM = 64
DIFF_V_DIM = 2 * DIFF_HEAD_DIM
POOL_WIDTH = 512
POOL_GROUPS = 4
POOL_GROUP_DIM = POOL_WIDTH // POOL_GROUPS
POOL_WINDOWS = (2, 4, 8, 16)
EVEN_IN = 3 * ATTN_WIDTH + POOL_WIDTH
CONV_WIDTH = 3
CONV_DIM = D_MODEL
EPS = 1e-6

LANES = 128
SUBLANES = 8
MIB = 1024 * 1024
VMEM_REQUEST_CAP = 63 * MIB

TOKEN_TILE = 512
FFN_TILE = 1024
MIXER_TILE = 1024
SUB_TILE = 512
FF_CHUNK = 256
Q_TILE = 256
ATTN_HEADS_PER_STEP = 2
POOL_HALO = max(POOL_WINDOWS)
POOL_PAD = SUBLANES
CONV_HALO = SUBLANES

F32 = jnp.float32
BF16 = jnp.bfloat16


def _vmem_limit(*nbytes):
    total = sum(nbytes)
    return int(min(VMEM_REQUEST_CAP, (total * 5 // 4 // MIB + 4) * MIB))


def _resident(shape):
    zeros = (0,) * len(shape)
    return pl.BlockSpec(shape, lambda *_: zeros, pipeline_mode=pl.Buffered(1))


def _resident_layer(shape, layer):
    index = (layer,) + (0,) * len(shape)
    return pl.BlockSpec((None,) + shape, lambda *_: index, pipeline_mode=pl.Buffered(1))


def _rms_scale(x):
    return lax.rsqrt(jnp.mean(x * x, axis=-1, keepdims=True) + EPS)


def _dot(a, b):
    return jnp.dot(a, b, preferred_element_type=F32)


def _ffn_kernel(*refs, mix_in):
    if mix_in:
        x_ref, a_ref, p_ref, wo_ref, g_ref, wg_ref, wu_ref, wd_ref, o_ref = refs
        x = (x_ref[...] + _dot(a_ref[...], wo_ref[0:ATTN_WIDTH, :].astype(BF16))
             + _dot(p_ref[...], wo_ref[ATTN_WIDTH:, :].astype(BF16)))
    else:
        x_ref, g_ref, wg_ref, wu_ref, wd_ref, o_ref = refs
        x = x_ref[...]
    xn = (x * _rms_scale(x) * g_ref[...]).astype(BF16)
    acc = jnp.zeros(x.shape, F32)
    for c in range(D_FF // FF_CHUNK):
        sl = slice(c * FF_CHUNK, (c + 1) * FF_CHUNK)
        hg = _dot(xn, wg_ref[:, sl].astype(BF16))
        hu = _dot(xn, wu_ref[:, sl].astype(BF16))
        a = (hg * jax.nn.sigmoid(hg) * hu).astype(BF16)
        acc = acc + _dot(a, wd_ref[sl, :].astype(BF16))
    o_ref[...] = x + 0.5 * acc


def _ffn(x, gain, w_gate, w_up, w_down, layer, mix=None):
    t, d = x.shape
    tm = FFN_TILE if mix is None else TOKEN_TILE
    row = lambda w: pl.BlockSpec((tm, w), lambda i: (i, 0))
    ffn_specs = [_resident_layer((1, d), layer), _resident_layer((d, D_FF), layer),
                 _resident_layer((d, D_FF), layer), _resident_layer((D_FF, d), layer)]
    ffn_args = (gain.reshape(DEPTH, 1, d), w_gate, w_up, w_down)
    if mix is None:
        mix_specs, mix_args, mix_bytes = [], (), 0
    else:
        attn, pool, w_out, mix_layer = mix
        mix_specs = [row(ATTN_WIDTH), row(POOL_WIDTH), _resident_layer((d, d), mix_layer)]
        mix_args = (attn, pool, w_out)
        mix_bytes = d * d * 4 + 2 * tm * d * 2
    return pl.pallas_call(
        functools.partial(_ffn_kernel, mix_in=mix is not None),
        grid=(t // tm,),
        in_specs=[row(d)] + mix_specs + ffn_specs,
        out_specs=row(d),
        out_shape=jax.ShapeDtypeStruct((t, d), F32),
        compiler_params=pltpu.CompilerParams(
            dimension_semantics=("parallel",),
            vmem_limit_bytes=_vmem_limit(3 * d * D_FF * 4, 4 * tm * d * 4, mix_bytes,
                                         2 * tm * d * 4, 3 * tm * FF_CHUNK * 4)),
        name="ffn_mix" if mix is not None else "ffn",
    )(x, *mix_args, *ffn_args)


def _even_in_kernel(x_ref, g_ref, w_ref, qkg_ref, wp_ref, ps_ref, qkv_ref, p_ref,
                    ubuf, *, tiles_per_seq):
    tm = x_ref.shape[0]
    sub = SUB_TILE
    n_sub = tm // sub
    lead = POOL_PAD + POOL_HALO

    @pl.when(pl.program_id(0) == 0)
    def _():
        for k in range(n_sub):
            ubuf[k, 0:lead, :] = jnp.zeros((lead, POOL_WIDTH), F32)

    seq_tile = pl.program_id(0) % tiles_per_seq
    lo = lax.broadcasted_iota(jnp.int32, (1, LANES), 1) < DIFF_HEAD_DIM
    history = jnp.where(seq_tile == 0, 0.0, ubuf[0, POOL_PAD:lead, :])
    for k in range(n_sub):
        rows = slice(k * sub, (k + 1) * sub)
        buf = ubuf.at[k]
        x = x_ref[rows, :]
        h = (x * _rms_scale(x) * g_ref[...]).astype(BF16)
        proj = _dot(h, w_ref[...].astype(BF16))

        for j in range(2 * ATTN_WIDTH // LANES):
            sl = slice(j * LANES, (j + 1) * LANES)
            blk = proj[:, sl]
            sq = blk * blk
            ss_lo = jnp.sum(jnp.where(lo, sq, 0.0), axis=-1, keepdims=True)
            ss_hi = jnp.sum(jnp.where(lo, 0.0, sq), axis=-1, keepdims=True)
            ms = jnp.where(lo, ss_lo, ss_hi) * (1.0 / DIFF_HEAD_DIM)
            qkv_ref[rows, sl] = (blk * lax.rsqrt(ms + EPS) * qkg_ref[:, sl]).astype(BF16)
        v_sl = slice(2 * ATTN_WIDTH, 3 * ATTN_WIDTH)
        qkv_ref[rows, v_sl] = proj[:, v_sl].astype(BF16)

        u = proj[:, 3 * ATTN_WIDTH:]
        buf[POOL_PAD:lead, :] = history
        buf[lead:lead + sub, :] = u
        history = u[sub - POOL_HALO:, :]
        pos = seq_tile * tm + k * sub + lax.broadcasted_iota(jnp.int32, (sub, 1), 0)
        for g, win in enumerate(POOL_WINDOWS):
            sl = slice(g * POOL_GROUP_DIM, (g + 1) * POOL_GROUP_DIM)
            w = 1
            while w < min(win, SUBLANES):
                buf[POOL_PAD:lead + sub, sl] = (buf[POOL_PAD:lead + sub, sl]
                                                + buf[POOL_PAD - w:lead + sub - w, sl])
                w *= 2
            acc = buf[lead:lead + sub, sl]
            while w < win:
                acc = acc + buf[lead - w:lead - w + sub, sl]
                w *= 2
            count = jnp.minimum(pos + 1, win).astype(F32)
            pooled = (acc / count - u[:, sl]).astype(BF16)
            y = _dot(pooled, wp_ref[g].astype(BF16))
            p_ref[rows, sl] = (y * ps_ref[:, sl]).astype(BF16)
    ubuf[0, POOL_PAD:lead, :] = history


def _even_in(x, gain, w_in, qk_gain, w_pool, pool_scale, seq_len, layer):
    t, d = x.shape
    tm = MIXER_TILE
    row = lambda w: pl.BlockSpec((tm, w), lambda i: (i, 0))
    return pl.pallas_call(
        functools.partial(_even_in_kernel, tiles_per_seq=seq_len // tm),
        grid=(t // tm,),
        in_specs=[row(d), _resident((1, d)), _resident_layer((d, EVEN_IN), layer),
                  _resident((1, 2 * ATTN_WIDTH)),
                  _resident_layer((POOL_GROUPS, POOL_GROUP_DIM, POOL_GROUP_DIM), layer),
                  _resident((1, POOL_WIDTH))],
        out_specs=[row(3 * ATTN_WIDTH), row(POOL_WIDTH)],
        out_shape=[jax.ShapeDtypeStruct((t, 3 * ATTN_WIDTH), BF16),
                   jax.ShapeDtypeStruct((t, POOL_WIDTH), BF16)],
        scratch_shapes=[pltpu.VMEM((tm // SUB_TILE, POOL_PAD + POOL_HALO + SUB_TILE,
                                    POOL_WIDTH), F32)],
        compiler_params=pltpu.CompilerParams(
            dimension_semantics=("arbitrary",),
            vmem_limit_bytes=_vmem_limit(d * EVEN_IN * 6, 2 * tm * d * 4,
                                         2 * tm * EVEN_IN * 2, 4 * SUB_TILE * EVEN_IN * 4)),
        name="even_in",
    )(x, gain, w_in, qk_gain, w_pool, pool_scale)


def _attn_kernel(lq1_ref, lk1_ref, lq2_ref, lk2_ref, sg_ref, q_ref, k_ref, v_ref, o_ref,
                 vext, *, lam_init):
    seq = q_ref.shape[0]
    heads = q_ref.shape[1] // LANES
    tq = Q_TILE
    n_blocks = seq // tq
    vw = DIFF_V_DIM + LANES
    lam = (jnp.exp(jnp.sum(lq1_ref[...] * lk1_ref[...], axis=-1, keepdims=True))
           - jnp.exp(jnp.sum(lq2_ref[...] * lk2_ref[...], axis=-1, keepdims=True))
           + lam_init)
    lo = lax.broadcasted_iota(jnp.int32, (1, LANES), 1) < DIFF_HEAD_DIM
    out_gain = sg_ref[...] * (1.0 - lam_init)
    contract_last = (((1,), (1,)), ((), ()))
    for h in range(heads):
        vext[:, h * vw:h * vw + DIFF_V_DIM] = v_ref[:, h * LANES:(h + 1) * LANES]
        vext[:, h * vw + DIFF_V_DIM:(h + 1) * vw] = jnp.ones((seq, LANES), BF16)

    rows = lax.broadcasted_iota(jnp.int32, (2 * tq, 1), 0)
    rows = jnp.where(rows >= tq, rows - tq, rows)
    causal = rows >= lax.broadcasted_iota(jnp.int32, (1, tq), 1)

    def block(h, i):
        past = i * tq
        hs = slice(h * LANES, (h + 1) * LANES)
        vs = slice(h * vw, (h + 1) * vw)
        q = q_ref[past:past + tq, hs]
        zero = jnp.zeros_like(q)
        qq = jnp.concatenate([jnp.where(lo, q, zero), jnp.where(lo, zero, q)], axis=0)
        s_d = lax.dot_general(qq, k_ref[past:past + tq, hs], contract_last,
                              preferred_element_type=F32)
        s_d = jnp.where(causal, s_d, -jnp.inf)
        m = jnp.max(s_d, axis=-1, keepdims=True)
        if i > 0:
            s_p = lax.dot_general(qq, k_ref[0:past, hs], contract_last,
                                  preferred_element_type=F32)
            m = jnp.maximum(m, jnp.max(s_p, axis=-1, keepdims=True))
        oe = _dot(jnp.exp2(s_d - m).astype(BF16), vext[past:past + tq, vs])
        if i > 0:
            oe = oe + _dot(jnp.exp2(s_p - m).astype(BF16), vext[0:past, vs])
        r = oe[:, 0:DIFF_V_DIM] / oe[:, DIFF_V_DIM:]
        o = r[0:tq, :] - lam * r[tq:, :]
        o_ref[past:past + tq, hs] = (o * _rms_scale(o) * out_gain).astype(BF16)

    for i in range(n_blocks):
        for h in range(heads):
            block(h, i if h % 2 == 0 else n_blocks - 1 - i)


def _attention(qkv, lq1, lk1, lq2, lk2, subln_gain, lam_init):
    b, s, _ = qkv.shape
    hps = ATTN_HEADS_PER_STEP
    width = hps * LANES
    groups = DIFF_HEADS // hps
    head = lambda off: pl.BlockSpec((None, s, width), lambda bi, gi: (bi, 0, off + gi))
    vec = lambda n: pl.BlockSpec((1, n), lambda bi, gi: (0, 0))
    return pl.pallas_call(
        functools.partial(_attn_kernel, lam_init=lam_init),
        grid=(b, groups),
        in_specs=[vec(DIFF_HEAD_DIM)] * 4 + [vec(DIFF_V_DIM), head(0), head(groups),
                                             head(2 * groups)],
        out_specs=pl.BlockSpec((None, s, width), lambda bi, gi: (bi, 0, gi)),
        out_shape=jax.ShapeDtypeStruct((b, s, ATTN_WIDTH), BF16),
        scratch_shapes=[pltpu.VMEM((s, hps * (DIFF_V_DIM + LANES)), BF16)],
        compiler_params=pltpu.CompilerParams(
            dimension_semantics=("parallel", "parallel"),
            vmem_limit_bytes=_vmem_limit(10 * s * width * 2, 8 * hps * Q_TILE * s * 4)),
        name="diff_attn",
    )(lq1, lk1, lq2, lk2, subln_gain, qkv, qkv, qkv)


def _odd_kernel(x_ref, g_ref, wi_ref, cw_ref, wo_ref, y_ref, zbuf, *, tiles_per_seq):
    tm = x_ref.shape[0]
    sub = SUB_TILE

    @pl.when(pl.program_id(0) == 0)
    def _():
        zbuf[0:CONV_HALO, :] = jnp.zeros((CONV_HALO, CONV_DIM), F32)

    seq_start = pl.program_id(0) % tiles_per_seq == 0
    zbuf[0:CONV_HALO, :] = jnp.where(seq_start, 0.0, zbuf[0:CONV_HALO, :])
    for k in range(tm // sub):
        rows = slice(k * sub, (k + 1) * sub)
        base = CONV_HALO + k * sub
        x = x_ref[rows, :]
        h = (x * _rms_scale(x) * g_ref[...]).astype(BF16)
        b_gate, c_gate, hh = [
            _dot(h, wi_ref[:, part * CONV_DIM:(part + 1) * CONV_DIM].astype(BF16))
            for part in range(3)]
        z = c_gate * hh
        zbuf[base:base + sub, :] = z
        conv = cw_ref[CONV_WIDTH - 1:CONV_WIDTH, :] * z
        for j in range(1, CONV_WIDTH):
            tap = cw_ref[CONV_WIDTH - 1 - j:CONV_WIDTH - j, :]
            conv = conv + tap * zbuf[base - j:base - j + sub, :]
        y_ref[rows, :] = x + _dot((b_gate * conv).astype(BF16), wo_ref[...].astype(BF16))
    zbuf[0:CONV_HALO, :] = z[sub - CONV_HALO:, :]


def _odd_mixer(x, gain, w_in, conv_w, w_out, seq_len, layer):
    t, d = x.shape
    tm = MIXER_TILE
    row = pl.BlockSpec((tm, d), lambda i: (i, 0))
    return pl.pallas_call(
        functools.partial(_odd_kernel, tiles_per_seq=seq_len // tm),
        grid=(t // tm,),
        in_specs=[row, _resident((1, d)), _resident_layer((d, 3 * CONV_DIM), layer),
                  _resident_layer((CONV_WIDTH, CONV_DIM), layer),
                  _resident_layer((CONV_DIM, d), layer)],
        out_specs=row,
        out_shape=jax.ShapeDtypeStruct((t, d), F32),
        scratch_shapes=[pltpu.VMEM((CONV_HALO + tm, CONV_DIM), F32)],
        compiler_params=pltpu.CompilerParams(
            dimension_semantics=("arbitrary",),
            vmem_limit_bytes=_vmem_limit(4 * d * CONV_DIM * 4, 4 * tm * d * 4,
                                         tm * d * 4, 8 * SUB_TILE * d * 4)),
        name="odd_mixer",
    )(x, gain, w_in, conv_w, w_out)


def kernel(x, ffn1_norm, ffn1_w_gate, ffn1_w_up, ffn1_w_down, ffn2_norm, ffn2_w_gate,
           ffn2_w_up, ffn2_w_down, mix_norm, even_w_in, even_w_out, q_gain, k_gain,
           subln_gain, lambda_q1, lambda_k1, lambda_q2, lambda_k2, w_pool, pool_scale,
           odd_w_in, conv_w, odd_w_out):
    batch, seq, d = x.shape
    assert d == D_MODEL and seq % MIXER_TILE == 0 and seq % Q_TILE == 0
    assert (batch * seq) % FFN_TILE == 0 and MIXER_TILE % SUB_TILE == 0
    assert SUB_TILE >= 2 * POOL_HALO
    vec = lambda v: v.reshape(1, -1).astype(F32)
    xt = x.reshape(batch * seq, d)
    for l in range(DEPTH):
        i = l // 2
        xt = _ffn(xt, ffn1_norm, ffn1_w_gate, ffn1_w_up, ffn1_w_down, l)
        if l % 2 == 0:
            lam_init = 0.8 - 0.6 * math.exp(-0.3 * l)
            reps = ATTN_WIDTH // DIFF_HEAD_DIM
            q_scale = math.log2(math.e) * DIFF_HEAD_DIM ** -0.5
            qk_gain = jnp.concatenate([jnp.tile(q_gain[i], reps) * q_scale,
                                       jnp.tile(k_gain[i], reps)])
            qkv, p = _even_in(xt, vec(mix_norm[l]), even_w_in, vec(qk_gain), w_pool,
                              vec(pool_scale[i]), seq, i)
            o = _attention(qkv.reshape(batch, seq, 3 * ATTN_WIDTH), vec(lambda_q1[i]),
                           vec(lambda_k1[i]), vec(lambda_q2[i]), vec(lambda_k2[i]),
                           vec(subln_gain[i]), lam_init)
            mix = (o.reshape(batch * seq, ATTN_WIDTH), p, even_w_out, i)
        else:
            xt = _odd_mixer(xt, vec(mix_norm[l]), odd_w_in, conv_w, odd_w_out, seq, i)
            mix = None
        xt = _ffn(xt, ffn2_norm, ffn2_w_gate, ffn2_w_up, ffn2_w_down, l, mix)
    return xt.reshape(batch, seq, d)
```
